```python
import math
import jax, jax.numpy as jnp
from jax import lax
import numpy as np

D_MODEL = 2048
BATCH = 1
SEQ = 8192
DEPTH = 1
DEC_BATCH = 32
DEC_SEQ = 1
PAST_LEN = 8192
PAGE_SIZE = 128

N_HEADS_A = 8
DQ_A = 64
DK_A = 2 * DQ_A
DV_A = 128
W_A = N_HEADS_A * DV_A
N_HEADS_B = 8
DH_B = 128
W_B = N_HEADS_B * DH_B
CHUNK = 128
D_FF = 5632
Q_BLOCK = 128
EPS = 1e-5
ATTN_SCALE = DQ_A ** -0.5
PROJ_SPLITS = (N_HEADS_A * DK_A, 2 * N_HEADS_A * DK_A, 2 * N_HEADS_A * DK_A + W_A,
               2 * N_HEADS_A * DK_A + W_A + W_B)
PROJ_COLS = 2 * N_HEADS_A * DK_A + W_A + 2 * W_B

kernel_name = "hymba_diffattn_chunkgmlp_macaron_step"


def rmsnorm(x, g):
    xf = x.astype(jnp.float32)
    y = xf * lax.rsqrt(jnp.mean(xf * xf, axis=-1, keepdims=True) + EPS)
    return (y * g.astype(jnp.float32)).astype(x.dtype)


def layernorm(x, g, b):
    xf = x.astype(jnp.float32)
    mu = jnp.mean(xf, axis=-1, keepdims=True)
    var = jnp.mean(jnp.square(xf - mu), axis=-1, keepdims=True)
    y = (xf - mu) * lax.rsqrt(var + EPS) * g.astype(jnp.float32) + b.astype(jnp.float32)
    return y.astype(x.dtype)


def swiglu_ffn(x, g, w_gate, w_up, w_down):
    h = rmsnorm(x, g)
    return (jax.nn.silu(h @ w_gate) * (h @ w_up)) @ w_down


def diff_attention(q, k, v, q_pos0, lam):
    b, sq, h = q.shape[0], q.shape[1], q.shape[2]
    sk = k.shape[1]
    qb = min(Q_BLOCK, sq)
    nb = sq // qb
    q_blocks = jnp.moveaxis(q.reshape(b, nb, qb, h, 2, DQ_A), 1, 0)
    k_pos = jnp.arange(sk)

    def one_block(args):
        q_blk, blk = args
        s = jnp.einsum('bqhmd,bkhmd->bhmqk', q_blk, k).astype(jnp.float32) * ATTN_SCALE
        q_pos = q_pos0 + blk * qb + jnp.arange(qb)
        s = jnp.where(k_pos[None, :] <= q_pos[:, None], s, -jnp.inf)
        p = jax.nn.softmax(s, axis=-1)
        w = p[:, :, 0] - lam * p[:, :, 1]
        return jnp.einsum('bhqk,bkhd->bqhd', w.astype(v.dtype), v)

    o = lax.map(one_block, (q_blocks, jnp.arange(nb)))
    return jnp.moveaxis(o, 0, 1).reshape(b, sq, h, DV_A)


def chunk_spatial(vn, ws, bs):
    b, s = vn.shape[0], vn.shape[1]
    L = min(CHUNK, s)
    n = s // L
    causal = jnp.tril(jnp.ones((L, L), dtype=bool))
    w = jnp.where(causal, ws[:, :L, :L], 0).astype(vn.dtype)
    vc = vn.reshape(b, n, L, N_HEADS_B, DH_B)
    out = jnp.einsum('hts,bnshd->bnthd', w, vc) + bs[:, :L].T[None, None, :, :, None].astype(vn.dtype)
    return out.reshape(b, s, N_HEADS_B, DH_B)


def trunk_layer(x, k_past, v_past, pos0, p, lam, lam_init):
    b, s = x.shape[0], x.shape[1]
    x = x + 0.5 * swiglu_ffn(x, p['ffn1_norm'], p['ffn1_w_gate'], p['ffn1_w_up'], p['ffn1_w_down'])
    h = rmsnorm(x, p['mix_norm'])
    proj = h @ p['w_in']
    q, k, v, u, g = jnp.split(proj, PROJ_SPLITS, axis=-1)
    k_new = k.reshape(b, s, N_HEADS_A, DK_A)
    v_new = v.reshape(b, s, N_HEADS_A, DV_A)
    if k_past is None:
        k_all, v_all = k_new, v_new
    else:
        k_all = jnp.concatenate([k_past, k_new], axis=1)
        v_all = jnp.concatenate([v_past, v_new], axis=1)
    qh = q.reshape(b, s, N_HEADS_A, 2, DQ_A)
    kh = k_all.reshape(b, k_all.shape[1], N_HEADS_A, 2, DQ_A)
    attn = diff_attention(qh, kh, v_all, pos0, lam)
    attn = (rmsnorm(attn, p['attn_subln']) * (1.0 - lam_init)).reshape(b, s, W_A)
    u = jax.nn.gelu(u)
    vn = layernorm(jax.nn.gelu(g), p['gmlp_ln_g'], p['gmlp_ln_b']).reshape(b, s, N_HEADS_B, DH_B)
    sg = chunk_spatial(vn, p['gmlp_ws'], p['gmlp_bs'])
    gm = rmsnorm(u * sg.reshape(b, s, W_B), p['gmlp_out_norm'])
    x = x + jnp.concatenate([attn, gm], axis=-1) @ p['w_out']
    x = x + 0.5 * swiglu_ffn(x, p['ffn2_norm'], p['ffn2_w_gate'], p['ffn2_w_up'], p['ffn2_w_down'])
    return x, k_new, v_new, vn


def setup_inputs(seed: int = 0) -> dict:
    key = jax.random.key(seed)
    ks = iter(jax.random.split(key, 40))
    f32 = jnp.float32
    n_pages = PAST_LEN // PAGE_SIZE
    n_used = DEC_BATCH * n_pages
    n_pool = (n_used * 5 + 3) // 4

    def nrm(shape, scale):
        return jax.random.normal(next(ks), shape, f32) * scale

    def gain(shape):
        return 1.0 + nrm(shape, 0.02)

    inp = {}
    inp['x_prompt'] = nrm((BATCH, SEQ, D_MODEL), 1.0)
    inp['x_sample'] = nrm((DEC_BATCH, DEC_SEQ, D_MODEL), 1.0)
    inp['cache_k'] = nrm((DEPTH, n_pool, PAGE_SIZE, N_HEADS_A, DK_A), 1.0)
    inp['cache_v'] = nrm((DEPTH, n_pool, PAGE_SIZE, N_HEADS_A, DV_A), 1.0)
    perm = jax.random.permutation(next(ks), n_pool)[:n_used]
    inp['page_table'] = perm.reshape(DEC_BATCH, n_pages).astype(jnp.int32)
    inp['ffn1_norm'] = gain((DEPTH, D_MODEL))
    inp['ffn1_w_gate'] = nrm((DEPTH, D_MODEL, D_FF), D_MODEL ** -0.5)
    inp['ffn1_w_up'] = nrm((DEPTH, D_MODEL, D_FF), D_MODEL ** -0.5)
    inp['ffn1_w_down'] = nrm((DEPTH, D_FF, D_MODEL), D_FF ** -0.5)
    inp['mix_norm'] = gain((DEPTH, D_MODEL))
    inp['w_in'] = nrm((DEPTH, D_MODEL, PROJ_COLS), D_MODEL ** -0.5)
    inp['lambda_q1'] = nrm((DEPTH, DQ_A), 0.1)
    inp['lambda_k1'] = nrm((DEPTH, DQ_A), 0.1)
    inp['lambda_q2'] = nrm((DEPTH, DQ_A), 0.1)
    inp['lambda_k2'] = nrm((DEPTH, DQ_A), 0.1)
    inp['attn_subln'] = gain((DEPTH, DV_A))
    inp['gmlp_ln_g'] = gain((DEPTH, W_B))
    inp['gmlp_ln_b'] = nrm((DEPTH, W_B), 0.02)
    inp['gmlp_ws'] = nrm((DEPTH, N_HEADS_B, CHUNK, CHUNK), CHUNK ** -0.5)
    inp['gmlp_bs'] = gain((DEPTH, N_HEADS_B, CHUNK))
    inp['gmlp_out_norm'] = gain((DEPTH, W_B))
    inp['w_out'] = nrm((DEPTH, W_A + W_B, D_MODEL), (W_A + W_B) ** -0.5)
    inp['ffn2_norm'] = gain((DEPTH, D_MODEL))
    inp['ffn2_w_gate'] = nrm((DEPTH, D_MODEL, D_FF), D_MODEL ** -0.5)
    inp['ffn2_w_up'] = nrm((DEPTH, D_MODEL, D_FF), D_MODEL ** -0.5)
    inp['ffn2_w_down'] = nrm((DEPTH, D_FF, D_MODEL), D_FF ** -0.5)
    inp['final_norm'] = gain((D_MODEL,))
    return inp


def reference(x_prompt, x_sample, cache_k, cache_v, page_table,
              ffn1_norm, ffn1_w_gate, ffn1_w_up, ffn1_w_down,
              mix_norm, w_in, lambda_q1, lambda_k1, lambda_q2, lambda_k2, attn_subln,
              gmlp_ln_g, gmlp_ln_b, gmlp_ws, gmlp_bs, gmlp_out_norm, w_out,
              ffn2_norm, ffn2_w_gate, ffn2_w_up, ffn2_w_down, final_norm):
    n_pages = PAST_LEN // PAGE_SIZE
    chunk_last = SEQ % CHUNK if SEQ % CHUNK else CHUNK
    xp, xs = x_prompt, x_sample
    kp_l, vp_l, ks_l, vs_l, cp_l, cs_l = [], [], [], [], [], []
    for l in range(DEPTH):
        p = dict(ffn1_norm=ffn1_norm[l], ffn1_w_gate=ffn1_w_gate[l], ffn1_w_up=ffn1_w_up[l],
                 ffn1_w_down=ffn1_w_down[l], mix_norm=mix_norm[l], w_in=w_in[l],
                 attn_subln=attn_subln[l], gmlp_ln_g=gmlp_ln_g[l], gmlp_ln_b=gmlp_ln_b[l],
                 gmlp_ws=gmlp_ws[l], gmlp_bs=gmlp_bs[l], gmlp_out_norm=gmlp_out_norm[l],
                 w_out=w_out[l], ffn2_norm=ffn2_norm[l], ffn2_w_gate=ffn2_w_gate[l],
                 ffn2_w_up=ffn2_w_up[l], ffn2_w_down=ffn2_w_down[l])
        lam_init = 0.8 - 0.6 * math.exp(-0.3 * l)
        lam = (jnp.exp(jnp.sum(lambda_q1[l].astype(jnp.float32) * lambda_k1[l].astype(jnp.float32)))
               - jnp.exp(jnp.sum(lambda_q2[l].astype(jnp.float32) * lambda_k2[l].astype(jnp.float32)))
               + lam_init)
        xp, kp, vp, cvp = trunk_layer(xp, None, None, 0, p, lam, lam_init)
        k_past = cache_k[l][page_table].reshape(DEC_BATCH, n_pages * PAGE_SIZE, N_HEADS_A, DK_A)
        v_past = cache_v[l][page_table].reshape(DEC_BATCH, n_pages * PAGE_SIZE, N_HEADS_A, DV_A)
        xs, ks, vs, cvs = trunk_layer(xs, k_past, v_past, PAST_LEN, p, lam, lam_init)
        kp_l.append(kp); vp_l.append(vp); ks_l.append(ks); vs_l.append(vs)
        cp_l.append(cvp[:, SEQ - chunk_last:]); cs_l.append(cvs)
    y_prompt = rmsnorm(xp, final_norm)
    y_sample = rmsnorm(xs, final_norm)
    return (y_prompt, y_sample, jnp.stack(kp_l), jnp.stack(vp_l), jnp.stack(ks_l), jnp.stack(vs_l),
            jnp.stack(cp_l), jnp.stack(cs_l))
```

```python
import functools

import jax
import jax.numpy as jnp
from jax import lax
from jax.experimental import pallas as pl
from jax.experimental.pallas import tpu as pltpu

D_MODEL = 2048
N_HEADS = 8
D_HEAD = 128
DQ = 64
W_HALF = N_HEADS * D_HEAD
D_FF = 5632
CHUNK = 128
PAGE = 128
EPS = 1e-5
ATTN_SCALE = DQ ** -0.5
LAM_INIT = 0.2
N_MAPS = 2 * N_HEADS

VMEM_LIMIT_BYTES = 56 * 1024 * 1024

F32 = jnp.float32
BF16 = jnp.bfloat16


def _rms_scale(x):
    return lax.rsqrt(jnp.mean(x * x, axis=-1, keepdims=True) + EPS)


def _ffn_kernel(x_ref, gin_ref, wg_ref, wu_ref, wd_ref, gout_ref, *refs, emit_x):
    if emit_x:
        xo_ref, hn_ref, h_scr, acc_scr = refs
    else:
        hn_ref, h_scr, acc_scr = refs
    f = pl.program_id(1)

    @pl.when(f == 0)
    def _():
        x = x_ref[...]
        h_scr[...] = (x * _rms_scale(x) * gin_ref[...]).astype(BF16)
        acc_scr[...] = jnp.zeros_like(acc_scr)

    h = h_scr[...]
    g = jnp.dot(h, wg_ref[...], preferred_element_type=F32)
    u = jnp.dot(h, wu_ref[...], preferred_element_type=F32)
    a = (g * jax.nn.sigmoid(g) * u).astype(BF16)
    acc_scr[...] += jnp.dot(a, wd_ref[...], preferred_element_type=F32)

    @pl.when(f == pl.num_programs(1) - 1)
    def _():
        xn = x_ref[...] + 0.5 * acc_scr[...]
        if emit_x:
            xo_ref[...] = xn
        hn_ref[...] = (xn * _rms_scale(xn) * gout_ref[...]).astype(hn_ref.dtype)


def _ffn(x, g_in, wg, wu, wd, g_out, *, tm, tf, emit_x, hn_dtype):
    m = x.shape[0]
    grid = (m // tm, D_FF // tf)
    row_spec = pl.BlockSpec((tm, D_MODEL), lambda i, f: (i, 0))
    gain_spec = pl.BlockSpec((1, D_MODEL), lambda i, f: (0, 0))
    out_shape = [jax.ShapeDtypeStruct((m, D_MODEL), hn_dtype)]
    out_specs = [row_spec]
    if emit_x:
        out_shape = [jax.ShapeDtypeStruct((m, D_MODEL), F32)] + out_shape
        out_specs = [row_spec] + out_specs
    return pl.pallas_call(
        functools.partial(_ffn_kernel, emit_x=emit_x),
        grid=grid,
        in_specs=[
            row_spec,
            gain_spec,
            pl.BlockSpec((D_MODEL, tf), lambda i, f: (0, f)),
            pl.BlockSpec((D_MODEL, tf), lambda i, f: (0, f)),
            pl.BlockSpec((tf, D_MODEL), lambda i, f: (f, 0)),
            gain_spec,
        ],
        out_specs=out_specs,
        out_shape=out_shape,
        scratch_shapes=[pltpu.VMEM((tm, D_MODEL), BF16), pltpu.VMEM((tm, D_MODEL), F32)],
        compiler_params=pltpu.CompilerParams(
            dimension_semantics=("parallel", "arbitrary"), vmem_limit_bytes=VMEM_LIMIT_BYTES),
        name="ffn",
    )(x, g_in, wg, wu, wd, g_out)


def _gating_norm(r, lng_ref, lnb_ref):
    t = jax.nn.gelu(r)
    mu = jnp.mean(t, axis=-1, keepdims=True)
    tc = t - mu
    var = jnp.mean(tc * tc, axis=-1, keepdims=True)
    return tc * lax.rsqrt(var + EPS) * lng_ref[...] + lnb_ref[...]


def _proj_prompt_kernel(h_ref, w_ref, lng_ref, lnb_ref,
                        q_ref, k_ref, kb_ref, v_ref, vt_ref, gu_ref, vn_ref, r_scr):
    j = pl.program_id(1)
    r_scr[...] = jnp.dot(h_ref[...], w_ref[...], preferred_element_type=F32)

    @pl.when(j == 0)
    def _():
        for hd in range(N_HEADS):
            q_ref[hd] = (r_scr[:, hd * D_HEAD:(hd + 1) * D_HEAD] * ATTN_SCALE).astype(BF16)

    @pl.when(j == 1)
    def _():
        k_ref[...] = r_scr[...]
        for hd in range(N_HEADS):
            kb_ref[hd] = r_scr[:, hd * D_HEAD:(hd + 1) * D_HEAD].astype(BF16)

    @pl.when(j == 2)
    def _():
        v_ref[...] = r_scr[...]
        for hd in range(N_HEADS):
            vt_ref[hd, 0] = r_scr[:, hd * D_HEAD:(hd + 1) * D_HEAD].T.astype(BF16)

    @pl.when(j == 3)
    def _():
        gu_ref[...] = jax.nn.gelu(r_scr[...])

    @pl.when(j == 4)
    def _():
        vn_ref[...] = _gating_norm(r_scr[...], lng_ref, lnb_ref)


def _proj_prompt(h, w_in, ln_g, ln_b, *, tm):
    s = h.shape[0]
    nblk = s // tm
    grid = (nblk, 5)
    flat = pl.BlockSpec((tm, W_HALF), lambda i, j: (i, 0))
    heads = pl.BlockSpec((N_HEADS, tm, D_HEAD), lambda i, j: (0, i, 0))
    vec = pl.BlockSpec((1, W_HALF), lambda i, j: (0, 0))
    return pl.pallas_call(
        _proj_prompt_kernel,
        grid=grid,
        in_specs=[
            pl.BlockSpec((tm, D_MODEL), lambda i, j: (i, 0)),
            pl.BlockSpec((D_MODEL, W_HALF), lambda i, j: (0, j)),
            vec, vec,
        ],
        out_specs=[
            heads,
            flat,
            heads,
            flat,
            pl.BlockSpec((N_HEADS, 1, D_HEAD, tm), lambda i, j: (0, i, 0, 0)),
            flat,
            flat,
        ],
        out_shape=[
            jax.ShapeDtypeStruct((N_HEADS, s, D_HEAD), BF16),
            jax.ShapeDtypeStruct((s, W_HALF), F32),
            jax.ShapeDtypeStruct((N_HEADS, s, D_HEAD), BF16),
            jax.ShapeDtypeStruct((s, W_HALF), F32),
            jax.ShapeDtypeStruct((N_HEADS, nblk, D_HEAD, tm), BF16),
            jax.ShapeDtypeStruct((s, W_HALF), F32),
            jax.ShapeDtypeStruct((s, W_HALF), F32),
        ],
        scratch_shapes=[pltpu.VMEM((tm, W_HALF), F32)],
        compiler_params=pltpu.CompilerParams(
            dimension_semantics=("parallel", "arbitrary"), vmem_limit_bytes=VMEM_LIMIT_BYTES),
        name="proj_prompt",
    )(h, w_in, ln_g, ln_b)


def _proj_sample_kernel(h_ref, w_ref, lng_ref, lnb_ref, q_ref, k_ref, v_ref, gu_ref, vn_ref):
    j = pl.program_id(0)
    r = jnp.dot(h_ref[...], w_ref[...], preferred_element_type=F32)

    @pl.when(j == 0)
    def _():
        q_ref[...] = r * ATTN_SCALE

    @pl.when(j == 1)
    def _():
        k_ref[...] = r

    @pl.when(j == 2)
    def _():
        v_ref[...] = r

    @pl.when(j == 3)
    def _():
        gu_ref[...] = jax.nn.gelu(r)

    @pl.when(j == 4)
    def _():
        vn_ref[...] = _gating_norm(r, lng_ref, lnb_ref)


def _proj_sample(h, w_in, ln_g, ln_b):
    b = h.shape[0]
    flat = pl.BlockSpec((b, W_HALF), lambda j: (0, 0))
    vec = pl.BlockSpec((1, W_HALF), lambda j: (0, 0))
    return pl.pallas_call(
        _proj_sample_kernel,
        grid=(5,),
        in_specs=[
            pl.BlockSpec((b, D_MODEL), lambda j: (0, 0)),
            pl.BlockSpec((D_MODEL, W_HALF), lambda j: (0, j)),
            vec, vec,
        ],
        out_specs=[flat] * 5,
        out_shape=[jax.ShapeDtypeStruct((b, W_HALF), F32)] * 5,
        compiler_params=pltpu.CompilerParams(
            dimension_semantics=("arbitrary",), vmem_limit_bytes=VMEM_LIMIT_BYTES),
        name="proj_sample",
    )(h, w_in, ln_g, ln_b)


def _lambda(lam_ref):
    lp = lam_ref[...]
    a = jnp.sum(lp[0:1] * lp[1:2], axis=-1, keepdims=True)
    b = jnp.sum(lp[2:3] * lp[3:4], axis=-1, keepdims=True)
    return jnp.exp(a) - jnp.exp(b) + LAM_INIT


def _attn_prompt_kernel(lam_ref, q_ref, k_ref, vt_ref, gsub_ref, o_ref,
                        q2_scr, acc_scr, m_scr, l_scr, *, tq):
    i = pl.program_id(1)
    q = q_ref[0].astype(F32)
    lane = lax.broadcasted_iota(jnp.int32, (tq, D_HEAD), 1)
    q2_scr[0:tq, :] = jnp.where(lane < DQ, q, 0.0).astype(BF16)
    q2_scr[tq:2 * tq, :] = jnp.where(lane >= DQ, q, 0.0).astype(BF16)
    m_scr[...] = jnp.full_like(m_scr, -jnp.inf)
    l_scr[...] = jnp.zeros_like(l_scr)
    acc_scr[...] = jnp.zeros_like(acc_scr)

    def block(kj, masked):
        k = k_ref[0, pl.ds(pl.multiple_of(kj * tq, tq), tq), :]
        st = lax.dot_general(k, q2_scr[...], (((1,), (1,)), ((), ())),
                             preferred_element_type=F32)
        if masked:
            row = lax.broadcasted_iota(jnp.int32, (tq, 2 * tq), 0)
            col = lax.broadcasted_iota(jnp.int32, (tq, 2 * tq), 1)
            qpos = jnp.where(col >= tq, col - tq, col)
            st = jnp.where(row <= qpos, st, -jnp.inf)
        m_old = m_scr[...]
        m_new = jnp.maximum(m_old, jnp.max(st, axis=0, keepdims=True))
        alpha = jnp.exp(m_old - m_new)
        p = jnp.exp(st - m_new)
        l_scr[...] = alpha * l_scr[...] + jnp.sum(p, axis=0, keepdims=True)
        acc_scr[...] = alpha * acc_scr[...] + jnp.dot(
            vt_ref[0, kj], p.astype(BF16), preferred_element_type=F32)
        m_scr[...] = m_new

    def body(kj, carry):
        block(kj, False)
        return carry

    lax.fori_loop(0, i, body, 0)
    block(i, True)

    lam = _lambda(lam_ref)
    inv_l = 1.0 / l_scr[...]
    acc = acc_scr[...]
    o = acc[:, 0:tq] * inv_l[:, 0:tq] - lam * (acc[:, tq:2 * tq] * inv_l[:, tq:2 * tq])
    ms = jnp.mean(o * o, axis=0, keepdims=True)
    y = (o * lax.rsqrt(ms + EPS) * gsub_ref[...]) * (1.0 - LAM_INIT)
    o_ref[...] = y.T.astype(o_ref.dtype)


def _attn_prompt(lam_p, q, kb, vt, gsub, *, tq):
    s = q.shape[1]
    nq = s // tq
    return pl.pallas_call(
        functools.partial(_attn_prompt_kernel, tq=tq),
        grid=(N_HEADS, nq),
        in_specs=[
            pl.BlockSpec((4, DQ), lambda h, i: (0, 0)),
            pl.BlockSpec((1, tq, D_HEAD), lambda h, i: (h, i, 0)),
            pl.BlockSpec((1, s, D_HEAD), lambda h, i: (h, 0, 0)),
            pl.BlockSpec((1, nq, D_HEAD, tq), lambda h, i: (h, 0, 0, 0)),
            pl.BlockSpec((D_HEAD, 1), lambda h, i: (0, 0)),
        ],
        out_specs=pl.BlockSpec((tq, D_HEAD), lambda h, i: (i, h)),
        out_shape=jax.ShapeDtypeStruct((s, W_HALF), BF16),
        scratch_shapes=[
            pltpu.VMEM((2 * tq, D_HEAD), BF16),
            pltpu.VMEM((D_HEAD, 2 * tq), F32),
            pltpu.VMEM((1, 2 * tq), F32),
            pltpu.VMEM((1, 2 * tq), F32),
        ],
        compiler_params=pltpu.CompilerParams(
            dimension_semantics=("parallel", "arbitrary"), vmem_limit_bytes=VMEM_LIMIT_BYTES),
        name="attn_prompt",
    )(lam_p, q, kb, vt, gsub)


def _attn_decode_kernel(pt_ref, lam_ref, q_ref, kn_ref, vn_ref, gsub_ref, *refs, pages_per_step):
    kp_refs = refs[:pages_per_step]
    vp_refs = refs[pages_per_step:2 * pages_per_step]
    o_ref, qr_scr, acc_scr, m_scr, l_scr = refs[2 * pages_per_step:]
    s = pl.program_id(1)
    rows_per_page = PAGE * N_HEADS

    @pl.when(s == 0)
    def _():
        q8 = q_ref[0]
        lane = lax.broadcasted_iota(jnp.int32, (N_HEADS, D_HEAD), 1)
        qr_scr[0:N_HEADS, :] = jnp.where(lane < DQ, q8, 0.0)
        qr_scr[N_HEADS:N_MAPS, :] = jnp.where(lane >= DQ, q8, 0.0)
        m_scr[...] = jnp.full_like(m_scr, -jnp.inf)
        l_scr[...] = jnp.zeros_like(l_scr)
        acc_scr[...] = jnp.zeros_like(acc_scr)

    qr = qr_scr[...]
    own_head = (lax.broadcasted_iota(jnp.int32, (N_MAPS, rows_per_page), 0) % N_HEADS
                == lax.broadcasted_iota(jnp.int32, (N_MAPS, rows_per_page), 1) % N_HEADS)
    sts = []
    for kp in kp_refs:
        k2d = kp[0].reshape(rows_per_page, D_HEAD)
        st = lax.dot_general(qr, k2d, (((1,), (1,)), ((), ())), preferred_element_type=F32)
        sts.append(jnp.where(own_head, st, -jnp.inf))
    m_old = m_scr[...]
    m_new = m_old
    for st in sts:
        m_new = jnp.maximum(m_new, jnp.max(st, axis=1, keepdims=True))
    alpha = jnp.exp(m_old - m_new)
    l_new = alpha * l_scr[...]
    acc = alpha * acc_scr[...]
    for st, vp in zip(sts, vp_refs):
        p = jnp.exp(st - m_new)
        l_new = l_new + jnp.sum(p, axis=1, keepdims=True)
        acc = acc + jnp.dot(p, vp[0].reshape(rows_per_page, D_HEAD), preferred_element_type=F32)
    m_scr[...] = m_new
    l_scr[...] = l_new
    acc_scr[...] = acc

    @pl.when(s == pl.num_programs(1) - 1)
    def _():
        kn16 = jnp.concatenate([kn_ref[0], kn_ref[0]], axis=0)
        vn16 = jnp.concatenate([vn_ref[0], vn_ref[0]], axis=0)
        s_new = jnp.sum(qr * kn16, axis=1, keepdims=True)
        m_fin = jnp.maximum(m_new, s_new)
        a_fin = jnp.exp(m_new - m_fin)
        p_new = jnp.exp(s_new - m_fin)
        l_fin = a_fin * l_new + p_new
        o16 = (a_fin * acc + p_new * vn16) / l_fin
        o = o16[0:N_HEADS] - _lambda(lam_ref) * o16[N_HEADS:N_MAPS]
        o_ref[0] = (o * _rms_scale(o) * gsub_ref[...]) * (1.0 - LAM_INIT)


def _attn_decode(page_table, lam_p, q, k_new, v_new, gsub_row, cache_k, cache_v, *, pages_per_step):
    b, n_pages = page_table.shape
    steps = n_pages // pages_per_step
    row = pl.BlockSpec((1, N_HEADS, D_HEAD), lambda bi, s, pt: (bi, 0, 0))

    def page_spec(i):
        return pl.BlockSpec((1, PAGE, N_HEADS, D_HEAD),
                            lambda bi, s, pt: (pt[bi, s * pages_per_step + i], 0, 0, 0))

    grid_spec = pltpu.PrefetchScalarGridSpec(
        num_scalar_prefetch=1,
        grid=(b, steps),
        in_specs=[
            pl.BlockSpec((4, DQ), lambda bi, s, pt: (0, 0)),
            row, row, row,
            pl.BlockSpec((1, D_HEAD), lambda bi, s, pt: (0, 0)),
        ] + [page_spec(i) for i in range(pages_per_step)] * 2,
        out_specs=row,
        scratch_shapes=[
            pltpu.VMEM((N_MAPS, D_HEAD), F32),
            pltpu.VMEM((N_MAPS, D_HEAD), F32),
            pltpu.VMEM((N_MAPS, 1), F32),
            pltpu.VMEM((N_MAPS, 1), F32),
        ],
    )
    heads = (b, N_HEADS, D_HEAD)
    return pl.pallas_call(
        functools.partial(_attn_decode_kernel, pages_per_step=pages_per_step),
        grid_spec=grid_spec,
        out_shape=jax.ShapeDtypeStruct(heads, F32),
        compiler_params=pltpu.CompilerParams(
            dimension_semantics=("parallel", "arbitrary"), vmem_limit_bytes=VMEM_LIMIT_BYTES),
        name="attn_decode",
    )(page_table, lam_p, q.reshape(heads), k_new.reshape(heads), v_new.reshape(heads), gsub_row,
      *([cache_k] * pages_per_step), *([cache_v] * pages_per_step))


def _mixout_prompt_kernel(attn_ref, gu_ref, vn_ref, x_ref, ws_ref, bias_ref, gout_ref, wo_ref,
                          xo_ref, t_scr, *, tm):
    row = lax.broadcasted_iota(jnp.int32, (CHUNK, CHUNK), 0)
    col = lax.broadcasted_iota(jnp.int32, (CHUNK, CHUNK), 1)
    for hd in range(N_HEADS):
        cols = slice(hd * D_HEAD, (hd + 1) * D_HEAD)
        w = jnp.where(row >= col, ws_ref[hd], 0.0).astype(BF16)
        for c in range(tm // CHUNK):
            rows = slice(c * CHUNK, (c + 1) * CHUNK)
            sg = jnp.dot(w, vn_ref[rows, cols].astype(BF16), preferred_element_type=F32) + bias_ref[:, cols]
            t_scr[rows, cols] = gu_ref[rows, cols] * sg
    t = t_scr[...]
    gm = (t * _rms_scale(t) * gout_ref[...]).astype(BF16)
    xo_ref[...] = (x_ref[...]
                   + jnp.dot(attn_ref[...], wo_ref[0:W_HALF, :], preferred_element_type=F32)
                   + jnp.dot(gm, wo_ref[W_HALF:2 * W_HALF, :], preferred_element_type=F32))


def _mixout_prompt(attn, gu, vn, x, ws, bias_tile, g_out, w_out, *, tm):
    s = x.shape[0]
    half = pl.BlockSpec((tm, W_HALF), lambda i: (i, 0))
    full = pl.BlockSpec((tm, D_MODEL), lambda i: (i, 0))
    return pl.pallas_call(
        functools.partial(_mixout_prompt_kernel, tm=tm),
        grid=(s // tm,),
        in_specs=[
            half, half, half, full,
            pl.BlockSpec((N_HEADS, CHUNK, CHUNK), lambda i: (0, 0, 0)),
            pl.BlockSpec((CHUNK, W_HALF), lambda i: (0, 0)),
            pl.BlockSpec((1, W_HALF), lambda i: (0, 0)),
            pl.BlockSpec((2 * W_HALF, D_MODEL), lambda i: (0, 0)),
        ],
        out_specs=full,
        out_shape=jax.ShapeDtypeStruct((s, D_MODEL), F32),
        scratch_shapes=[pltpu.VMEM((tm, W_HALF), F32)],
        compiler_params=pltpu.CompilerParams(
            dimension_semantics=("parallel",), vmem_limit_bytes=VMEM_LIMIT_BYTES),
        name="mixout_prompt",
    )(attn, gu, vn, x, ws, bias_tile, g_out, w_out)


def _mixout_sample_kernel(attn_ref, gu_ref, vn_ref, x_ref, wdiag_ref, bias_ref, gout_ref, wo_ref, xo_ref):
    t = gu_ref[...] * (vn_ref[...] * wdiag_ref[...] + bias_ref[...])
    gm = (t * _rms_scale(t) * gout_ref[...]).astype(BF16)
    xo_ref[...] = (x_ref[...]
                   + jnp.dot(attn_ref[...].astype(BF16), wo_ref[0:W_HALF, :], preferred_element_type=F32)
                   + jnp.dot(gm, wo_ref[W_HALF:2 * W_HALF, :], preferred_element_type=F32))


def _mixout_sample(attn, gu, vn, x, wdiag_row, bias_row, g_out, w_out):
    b = x.shape[0]
    half = pl.BlockSpec((b, W_HALF), lambda i: (0, 0))
    vec = pl.BlockSpec((1, W_HALF), lambda i: (0, 0))
    full = pl.BlockSpec((b, D_MODEL), lambda i: (0, 0))
    return pl.pallas_call(
        _mixout_sample_kernel,
        grid=(1,),
        in_specs=[half, half, half, full, vec, vec, vec,
                  pl.BlockSpec((2 * W_HALF, D_MODEL), lambda i: (0, 0))],
        out_specs=full,
        out_shape=jax.ShapeDtypeStruct((b, D_MODEL), F32),
        compiler_params=pltpu.CompilerParams(
            dimension_semantics=("arbitrary",), vmem_limit_bytes=VMEM_LIMIT_BYTES),
        name="mixout_sample",
    )(attn, gu, vn, x, wdiag_row, bias_row, g_out, w_out)


def kernel(x_prompt, x_sample, cache_k, cache_v, page_table, ffn1_norm, ffn1_w_gate, ffn1_w_up,
           ffn1_w_down, mix_norm, w_in, lambda_q1, lambda_k1, lambda_q2, lambda_k2, attn_subln,
           gmlp_ln_g, gmlp_ln_b, gmlp_ws, gmlp_bs, gmlp_out_norm, w_out,
           ffn2_norm, ffn2_w_gate, ffn2_w_up, ffn2_w_down, final_norm):
    assert ffn1_norm.shape[0] == 1, "single-layer trunk"
    batch, seq, _ = x_prompt.shape
    dec_batch, dec_seq, _ = x_sample.shape
    assert batch == 1 and dec_seq == 1 and seq % CHUNK == 0

    wg1, wu1, wd1 = (w[0].astype(BF16) for w in (ffn1_w_gate, ffn1_w_up, ffn1_w_down))
    wg2, wu2, wd2 = (w[0].astype(BF16) for w in (ffn2_w_gate, ffn2_w_up, ffn2_w_down))
    w_in_b = w_in[0].astype(BF16)
    w_out_b = w_out[0].astype(BF16)
    g_ffn1, g_mix, g_ffn2 = ffn1_norm, mix_norm, ffn2_norm
    g_final = final_norm.reshape(1, D_MODEL)
    ln_g, ln_b, g_gout = gmlp_ln_g, gmlp_ln_b, gmlp_out_norm
    lam_p = jnp.concatenate([lambda_q1, lambda_k1, lambda_q2, lambda_k2], axis=0)
    gsub_col = attn_subln.reshape(D_HEAD, 1)
    gsub_row = attn_subln.reshape(1, D_HEAD)
    ws, bs = gmlp_ws[0], gmlp_bs[0]
    bias_tile = jnp.repeat(bs.T, D_HEAD, axis=1)
    wdiag_row = jnp.repeat(ws[:, 0, 0], D_HEAD).reshape(1, W_HALF)
    bias_row = jnp.repeat(bs[:, 0], D_HEAD).reshape(1, W_HALF)

    xp = x_prompt.reshape(seq, D_MODEL)
    x1, h2 = _ffn(xp, g_ffn1, wg1, wu1, wd1, g_mix, tm=512, tf=512, emit_x=True, hn_dtype=BF16)
    q, k_f, k_b, v_f, v_t, gu, vn = _proj_prompt(h2, w_in_b, ln_g, ln_b, tm=512)
    attn = _attn_prompt(lam_p, q, k_b, v_t, gsub_col, tq=512)
    x2 = _mixout_prompt(attn, gu, vn, x1, ws, bias_tile, g_gout, w_out_b, tm=512)
    (y_p,) = _ffn(x2, g_ffn2, wg2, wu2, wd2, g_final, tm=512, tf=512, emit_x=False, hn_dtype=F32)

    xs = x_sample.reshape(dec_batch, D_MODEL)
    x1s, h2s = _ffn(xs, g_ffn1, wg1, wu1, wd1, g_mix, tm=dec_batch, tf=512, emit_x=True, hn_dtype=BF16)
    qs, ks, vs, gus, vns = _proj_sample(h2s, w_in_b, ln_g, ln_b)
    attn_s = _attn_decode(page_table, lam_p, qs, ks, vs, gsub_row, cache_k[0], cache_v[0],
                          pages_per_step=8)
    x2s = _mixout_sample(attn_s.reshape(dec_batch, W_HALF), gus, vns, x1s, wdiag_row, bias_row,
                         g_gout, w_out_b)
    (y_s,) = _ffn(x2s, g_ffn2, wg2, wu2, wd2, g_final, tm=dec_batch, tf=512, emit_x=False, hn_dtype=F32)

    return (
        y_p.reshape(batch, seq, D_MODEL),
        y_s.reshape(dec_batch, dec_seq, D_MODEL),
        k_f.reshape(1, batch, seq, N_HEADS, D_HEAD),
        v_f.reshape(1, batch, seq, N_HEADS, D_HEAD),
        ks.reshape(1, dec_batch, dec_seq, N_HEADS, D_HEAD),
        vs.reshape(1, dec_batch, dec_seq, N_HEADS, D_HEAD),
        vn[seq - CHUNK:].reshape(1, batch, CHUNK, N_HEADS, D_HEAD),
        vns.reshape(1, dec_batch, dec_seq, N_HEADS, D_HEAD),
    )
```
